```python
import jax, jax.numpy as jnp
from jax import lax
import numpy as np

D_MODEL = 2048
BATCH = 4
SEQ = 2048
DEPTH = 2
DEC_BATCH = 128
DEC_SEQ = 8
PAST_LEN = 16384
PAGE_SIZE = 128

D_MIX = 2 * D_MODEL
D_CONV = D_MIX // 2
D_SSM = D_MIX - D_CONV
CONF_K = 31
SSM_HEAD_DIM = 64
SSM_HEADS = D_SSM // SSM_HEAD_DIM
SSM_GROUPS = 4
SSM_STATE = 128
SSM_CONV_K = 4
SSD_CHUNK = 128
D_XBC = D_SSM + 2 * SSM_GROUPS * SSM_STATE
D_IN = 2 * D_CONV + D_SSM + D_XBC + SSM_HEADS
D_FF = 5632
FFN_CONV_K = 3
N_MOD = 6
EPS = 1e-6

kernel_name = "hymba_conformer_ssd_convffn_adaln_step"


def rmsnorm(x, w):
    x32 = x.astype(jnp.float32)
    y = x32 * lax.rsqrt(jnp.mean(x32 * x32, axis=-1, keepdims=True) + EPS)
    return (y * w.astype(jnp.float32)).astype(x.dtype)


def layernorm(x, w, b):
    x32 = x.astype(jnp.float32)
    mu = jnp.mean(x32, axis=-1, keepdims=True)
    xc = x32 - mu
    y = xc * lax.rsqrt(jnp.mean(xc * xc, axis=-1, keepdims=True) + 1e-5)
    return (y * w.astype(jnp.float32) + b.astype(jnp.float32)).astype(x.dtype)


def causal_dwconv(u, buf, w, b):
    k = w.shape[0]
    full = jnp.concatenate([buf.astype(u.dtype), u], axis=1)
    out = lax.conv_general_dilated(
        full, w[:, None, :].astype(u.dtype), window_strides=(1,), padding='VALID',
        dimension_numbers=('NWC', 'WIO', 'NWC'), feature_group_count=u.shape[-1])
    new_buf = full[:, full.shape[1] - (k - 1):]
    return out + b.astype(u.dtype), new_buf


def ssd_scan(x, dt, a, bm, cm, d_skip, h0, chunk):
    bsz, L, H, P = x.shape
    G, N = bm.shape[2], bm.shape[3]
    R = H // G
    nc = L // chunk
    f32 = jnp.float32
    xr = x.reshape(bsz, nc, chunk, G, R, P).astype(f32)
    dtr = dt.reshape(bsz, nc, chunk, G, R)
    br = bm.reshape(bsz, nc, chunk, G, N).astype(f32)
    cr = cm.reshape(bsz, nc, chunk, G, N).astype(f32)
    acs = jnp.cumsum(dtr * a.reshape(G, R).astype(f32), axis=2)
    xdt = xr * dtr[..., None]
    seg = acs[:, :, :, None] - acs[:, :, None, :]
    causal = jnp.tril(jnp.ones((chunk, chunk), dtype=bool))[None, None, :, :, None, None]
    lmat = jnp.exp(jnp.where(causal, seg, -jnp.inf))
    cb = jnp.einsum('bcign,bcjgn->bcijg', cr, br)
    y_diag = jnp.einsum('bcijg,bcijgr,bcjgrp->bcigrp', cb, lmat, xdt)
    decay_end = jnp.exp(acs[:, :, -1:] - acs)
    chunk_states = jnp.einsum('bcjgn,bcjgr,bcjgrp->bcgrpn', br, decay_end, xdt)
    chunk_decay = jnp.exp(acs[:, :, -1])

    def step(h, inp):
        dec, st = inp
        return h * dec[..., None, None] + st, h

    h_fin, h_prev = lax.scan(step, h0.reshape(bsz, G, R, P, N).astype(f32),
                             (jnp.moveaxis(chunk_decay, 1, 0), jnp.moveaxis(chunk_states, 1, 0)))
    h_prev = jnp.moveaxis(h_prev, 0, 1)
    y_off = jnp.einsum('bcign,bcigr,bcgrpn->bcigrp', cr, jnp.exp(acs), h_prev)
    y = y_diag + y_off + xr * d_skip.reshape(G, R).astype(f32)[:, :, None]
    return y.reshape(bsz, L, H * P).astype(x.dtype), h_fin.reshape(bsz, H, P, N).astype(h0.dtype)


def hybrid_mixer(h, st_conf, st_sconv, st_ssm, w_in, conf_dw_w, conf_dw_b, conf_ln_w, conf_ln_b,
                 ssm_conv_w, ssm_conv_b, dt_bias, a_log, d_skip, ssm_norm_w, w_out, chunk):
    bsz, L, _ = h.shape
    p = h @ w_in
    o1 = 2 * D_CONV
    o2 = o1 + D_SSM
    o3 = o2 + D_XBC
    conf_in, z, xbc, dt_raw = p[..., :o1], p[..., o1:o2], p[..., o2:o3], p[..., o3:]
    u = conf_in[..., :D_CONV] * jax.nn.sigmoid(conf_in[..., D_CONV:])
    v, new_conf = causal_dwconv(u, st_conf, conf_dw_w, conf_dw_b)
    v = jax.nn.silu(layernorm(v, conf_ln_w, conf_ln_b))
    xbc, new_sconv = causal_dwconv(xbc, st_sconv, ssm_conv_w, ssm_conv_b)
    xbc = jax.nn.silu(xbc)
    gn = SSM_GROUPS * SSM_STATE
    xs = xbc[..., :D_SSM].reshape(bsz, L, SSM_HEADS, SSM_HEAD_DIM)
    bm = xbc[..., D_SSM:D_SSM + gn].reshape(bsz, L, SSM_GROUPS, SSM_STATE)
    cm = xbc[..., D_SSM + gn:].reshape(bsz, L, SSM_GROUPS, SSM_STATE)
    dt = jax.nn.softplus(dt_raw.astype(jnp.float32) + dt_bias.astype(jnp.float32))
    a = -jnp.exp(a_log.astype(jnp.float32))
    y, new_ssm = ssd_scan(xs, dt, a, bm, cm, d_skip, st_ssm, chunk)
    y = rmsnorm(y * jax.nn.silu(z), ssm_norm_w)
    out = jnp.concatenate([v, y], axis=-1) @ w_out
    return out, new_conf, new_sconv, new_ssm


def conv_ffn(h, st_f, w_up, dw_w, dw_b, w_down):
    up, new_f = causal_dwconv(h @ w_up, st_f, dw_w, dw_b)
    act = up[..., :D_FF] * jax.nn.silu(up[..., D_FF:])
    return act @ w_down, new_f


def run_trunk(x, c, st_conf, st_sconv, st_ssm, st_ffn, params):
    (ada_w, ada_b, norm_mix_w, norm_ffn_w, w_in, conf_dw_w, conf_dw_b, conf_ln_w, conf_ln_b,
     ssm_conv_w, ssm_conv_b, dt_bias, a_log, d_skip, ssm_norm_w, w_out,
     ffn_w_up, ffn_dw_w, ffn_dw_b, ffn_w_down, final_norm_w) = params
    L = x.shape[1]
    chunk = SSD_CHUNK if L % SSD_CHUNK == 0 else L
    new_conf, new_sconv, new_ssm, new_ffn = [], [], [], []
    for l in range(DEPTH):
        mod = jax.nn.silu(c) @ ada_w[l] + ada_b[l]
        sh_m, sc_m, g_m, sh_f, sc_f, g_f = jnp.split(mod[:, None, :], N_MOD, axis=-1)
        h = rmsnorm(x, norm_mix_w[l]) * (1 + sc_m) + sh_m
        out, nc_, ns_, nh_ = hybrid_mixer(
            h, st_conf[l], st_sconv[l], st_ssm[l], w_in[l], conf_dw_w[l], conf_dw_b[l],
            conf_ln_w[l], conf_ln_b[l], ssm_conv_w[l], ssm_conv_b[l], dt_bias[l], a_log[l],
            d_skip[l], ssm_norm_w[l], w_out[l], chunk)
        x = x + g_m * out
        h = rmsnorm(x, norm_ffn_w[l]) * (1 + sc_f) + sh_f
        f, nf_ = conv_ffn(h, st_ffn[l], ffn_w_up[l], ffn_dw_w[l], ffn_dw_b[l], ffn_w_down[l])
        x = x + g_f * f
        new_conf.append(nc_)
        new_sconv.append(ns_)
        new_ssm.append(nh_)
        new_ffn.append(nf_)
    y = rmsnorm(x, final_norm_w)
    return y, jnp.stack(new_conf), jnp.stack(new_sconv), jnp.stack(new_ssm), jnp.stack(new_ffn)


def setup_inputs(seed: int = 0) -> dict:
    key = jax.random.key(seed)
    ks = jax.random.split(key, 32)
    f32 = jnp.float32

    def nrm(k, shape, scale):
        return jax.random.normal(k, shape, f32) * scale

    dt0 = jnp.exp(jax.random.uniform(ks[20], (DEPTH, SSM_HEADS), f32,
                                     jnp.log(0.001), jnp.log(0.1)))
    dt_bias = dt0 + jnp.log(-jnp.expm1(-dt0))
    a_log = jnp.log(jax.random.uniform(ks[21], (DEPTH, SSM_HEADS), f32, 1.0, 16.0))
    return {
        "x_prompt": nrm(ks[0], (BATCH, SEQ, D_MODEL), 1.0),
        "x_sample": nrm(ks[1], (DEC_BATCH, DEC_SEQ, D_MODEL), 1.0),
        "c_prompt": nrm(ks[2], (BATCH, D_MODEL), 1.0),
        "c_sample": nrm(ks[3], (DEC_BATCH, D_MODEL), 1.0),
        "state_conf_conv": nrm(ks[4], (DEPTH, DEC_BATCH, CONF_K - 1, D_CONV), 0.5),
        "state_ssm_conv": nrm(ks[5], (DEPTH, DEC_BATCH, SSM_CONV_K - 1, D_XBC), 1.0),
        "state_ssm": nrm(ks[6], (DEPTH, DEC_BATCH, SSM_HEADS, SSM_HEAD_DIM, SSM_STATE), 0.05),
        "state_ffn_conv": nrm(ks[7], (DEPTH, DEC_BATCH, FFN_CONV_K - 1, 2 * D_FF), 1.0),
        "ada_w": nrm(ks[8], (DEPTH, D_MODEL, N_MOD * D_MODEL), 0.5 * D_MODEL ** -0.5),
        "ada_b": nrm(ks[9], (DEPTH, N_MOD * D_MODEL), 0.02),
        "norm_mix_w": 1.0 + nrm(ks[10], (DEPTH, D_MODEL), 0.02),
        "norm_ffn_w": 1.0 + nrm(ks[11], (DEPTH, D_MODEL), 0.02),
        "w_in": nrm(ks[12], (DEPTH, D_MODEL, D_IN), D_MODEL ** -0.5),
        "conf_dw_w": nrm(ks[13], (DEPTH, CONF_K, D_CONV), CONF_K ** -0.5),
        "conf_dw_b": nrm(ks[14], (DEPTH, D_CONV), 0.02),
        "conf_ln_w": 1.0 + nrm(ks[15], (DEPTH, D_CONV), 0.02),
        "conf_ln_b": nrm(ks[16], (DEPTH, D_CONV), 0.02),
        "ssm_conv_w": nrm(ks[17], (DEPTH, SSM_CONV_K, D_XBC), SSM_CONV_K ** -0.5),
        "ssm_conv_b": nrm(ks[18], (DEPTH, D_XBC), 0.02),
        "dt_bias": dt_bias,
        "a_log": a_log,
        "d_skip": 1.0 + nrm(ks[19], (DEPTH, SSM_HEADS), 0.1),
        "ssm_norm_w": 1.0 + nrm(ks[22], (DEPTH, D_SSM), 0.02),
        "w_out": nrm(ks[23], (DEPTH, D_MIX, D_MODEL), D_MIX ** -0.5),
        "ffn_w_up": nrm(ks[24], (DEPTH, D_MODEL, 2 * D_FF), D_MODEL ** -0.5),
        "ffn_dw_w": nrm(ks[25], (DEPTH, FFN_CONV_K, 2 * D_FF), FFN_CONV_K ** -0.5),
        "ffn_dw_b": nrm(ks[26], (DEPTH, 2 * D_FF), 0.02),
        "ffn_w_down": nrm(ks[27], (DEPTH, D_FF, D_MODEL), D_FF ** -0.5),
        "final_norm_w": 1.0 + nrm(ks[28], (D_MODEL,), 0.02),
    }


def reference(x_prompt, x_sample, c_prompt, c_sample, state_conf_conv, state_ssm_conv, state_ssm,
              state_ffn_conv, ada_w, ada_b, norm_mix_w, norm_ffn_w, w_in, conf_dw_w, conf_dw_b,
              conf_ln_w, conf_ln_b, ssm_conv_w, ssm_conv_b, dt_bias, a_log, d_skip, ssm_norm_w,
              w_out, ffn_w_up, ffn_dw_w, ffn_dw_b, ffn_w_down, final_norm_w):
    params = (ada_w, ada_b, norm_mix_w, norm_ffn_w, w_in, conf_dw_w, conf_dw_b, conf_ln_w,
              conf_ln_b, ssm_conv_w, ssm_conv_b, dt_bias, a_log, d_skip, ssm_norm_w, w_out,
              ffn_w_up, ffn_dw_w, ffn_dw_b, ffn_w_down, final_norm_w)
    bp = x_prompt.shape[0]
    dtp = x_prompt.dtype
    z_conf = jnp.zeros((DEPTH, bp, CONF_K - 1, D_CONV), dtp)
    z_sconv = jnp.zeros((DEPTH, bp, SSM_CONV_K - 1, D_XBC), dtp)
    z_ssm = jnp.zeros((DEPTH, bp, SSM_HEADS, SSM_HEAD_DIM, SSM_STATE), state_ssm.dtype)
    z_ffn = jnp.zeros((DEPTH, bp, FFN_CONV_K - 1, 2 * D_FF), dtp)
    y_prompt, p_conf, p_sconv, p_ssm, p_ffn = run_trunk(
        x_prompt, c_prompt, z_conf, z_sconv, z_ssm, z_ffn, params)
    y_sample, s_conf, s_sconv, s_ssm, s_ffn = run_trunk(
        x_sample, c_sample, state_conf_conv, state_ssm_conv, state_ssm, state_ffn_conv, params)
    return (y_prompt, y_sample, p_conf, p_sconv, p_ssm, p_ffn, s_conf, s_sconv, s_ssm, s_ffn)
```

```python
import functools
import math

import jax
import jax.numpy as jnp
from jax import lax
from jax.experimental import pallas as pl
from jax.experimental.pallas import tpu as pltpu

F32 = jnp.float32
BF16 = jnp.bfloat16

RMS_EPS = 1e-6
LN_EPS = 1e-5
SSM_HEAD_DIM = 64
SSM_GROUPS = 4
SSM_STATE = 128
SSD_CHUNK = 128
N_MOD = 6

SUBLANES = 8
LANES = 128
TOKEN_TILE = 512
CONV_ROWS = 64
VMEM_LIMIT_BYTES = 56 * 1024 * 1024


def _round_up(n, m):
    return (n + m - 1) // m * m


def _params(n_axes):
    return pltpu.CompilerParams(dimension_semantics=("arbitrary",) * n_axes,
                                vmem_limit_bytes=VMEM_LIMIT_BYTES)


def _silu(v):
    return v * jax.nn.sigmoid(v)


class Geo:
    def __init__(self, B, L, tile):
        if L >= tile:
            assert L % tile == 0
            self.bb, self.lb = 1, tile
        else:
            assert L % SUBLANES == 0 and tile % L == 0 and B % (tile // L) == 0
            self.bb, self.lb = tile // L, L
        self.B, self.L = B, L
        self.tm = self.bb * self.lb
        self.nt = L // self.lb
        self.nb = B // self.bb
        self.n_tiles = self.nb * self.nt

    def bi(self, i):
        return i // self.nt

    def ti(self, i):
        return i % self.nt


def _ada_kernel(c_ref, w_ref, b_ref, o_ref):
    a = _silu(c_ref[...]).astype(BF16)
    o_ref[0] = jnp.dot(a, w_ref[0].astype(BF16), preferred_element_type=F32) + b_ref[0]


def _ada(c_all, ada_w, ada_b):
    depth, d, n = ada_w.shape
    bp = c_all.shape[0]
    tn = min(1024, n)
    return pl.pallas_call(
        _ada_kernel,
        out_shape=jax.ShapeDtypeStruct((depth, bp, n), F32),
        grid=(depth, n // tn),
        in_specs=[pl.BlockSpec((bp, d), lambda l, j: (0, 0)),
                  pl.BlockSpec((1, d, tn), lambda l, j: (l, 0, j)),
                  pl.BlockSpec((1, 1, tn), lambda l, j: (l, 0, j))],
        out_specs=pl.BlockSpec((1, bp, tn), lambda l, j: (l, 0, j)),
        compiler_params=_params(2), name="ada_mod",
    )(c_all, ada_w, ada_b.reshape(depth, 1, n))


def _normmod_kernel(x_ref, w_ref, sc_ref, sh_ref, o_ref):
    x = x_ref[...]
    y = x * lax.rsqrt(jnp.mean(x * x, axis=-1, keepdims=True) + RMS_EPS) * w_ref[...]
    h = y * (1.0 + sc_ref[...]) + sh_ref[...]
    o_ref[...] = h.reshape(o_ref.shape).astype(o_ref.dtype)


def _normmod(x, w, mod, sc_idx, sh_idx, geo):
    B, L, D = x.shape
    return pl.pallas_call(
        _normmod_kernel,
        out_shape=jax.ShapeDtypeStruct((B * L, D), BF16),
        grid=(geo.n_tiles,),
        in_specs=[pl.BlockSpec((geo.bb, geo.lb, D), lambda i: (geo.bi(i), geo.ti(i), 0)),
                  pl.BlockSpec((1, D), lambda i: (0, 0)),
                  pl.BlockSpec((geo.bb, 1, D), lambda i: (geo.bi(i), 0, sc_idx)),
                  pl.BlockSpec((geo.bb, 1, D), lambda i: (geo.bi(i), 0, sh_idx))],
        out_specs=pl.BlockSpec((geo.tm, D), lambda i: (i, 0)),
        compiler_params=_params(1), name="norm_mod",
    )(x, w.reshape(1, D), mod, mod)


def _final_norm_kernel(x_ref, w_ref, o_ref):
    x = x_ref[...]
    o_ref[...] = x * lax.rsqrt(jnp.mean(x * x, axis=-1, keepdims=True) + RMS_EPS) * w_ref[...]


def _final_norm(x, w, geo):
    B, L, D = x.shape
    spec = pl.BlockSpec((geo.bb, geo.lb, D), lambda i: (geo.bi(i), geo.ti(i), 0))
    return pl.pallas_call(
        _final_norm_kernel,
        out_shape=jax.ShapeDtypeStruct((B, L, D), F32),
        grid=(geo.n_tiles,),
        in_specs=[spec, pl.BlockSpec((1, D), lambda i: (0, 0))],
        out_specs=spec,
        compiler_params=_params(1), name="final_norm",
    )(x, w.reshape(1, D))


def _conv_long(buf, u, cw_ref, cb_ref, ns_ref, ti, nt, taps, tm):
    pad = _round_up(taps - 1, SUBLANES)

    @pl.when(ti == 0)
    def _():
        buf[0:pad, :] = jnp.zeros((pad, buf.shape[1]), F32)

    buf[pad:pad + tm, :] = u
    ch = min(CONV_ROWS, tm)
    outs = []
    for r0 in range(0, tm, ch):
        acc = cb_ref[...] + cw_ref[taps - 1:taps, :] * buf[pad + r0:pad + r0 + ch, :]
        for k in range(taps - 1):
            off = pad - (taps - 1) + k + r0
            acc = acc + cw_ref[k:k + 1, :] * buf[off:off + ch, :]
        outs.append(acc)
    tail = buf[tm:tm + pad, :]
    buf[0:pad, :] = tail

    @pl.when(ti == nt - 1)
    def _():
        ns_ref[0] = buf[pad - (taps - 1):pad, :]

    return outs[0] if len(outs) == 1 else jnp.concatenate(outs, axis=0)


def _conv_short(buf, u, st_ref, cw_ref, cb_ref, ns_ref, taps, bb, lb):
    buf[:, 0:taps - 1, :] = st_ref[...]
    buf[:, taps - 1:taps - 1 + lb, :] = u.reshape(bb, lb, u.shape[-1])
    ch = max(1, min(bb, CONV_ROWS // lb))
    outs = []
    for b0 in range(0, bb, ch):
        acc = cb_ref[...][None] + cw_ref[0:1, :][None] * buf[b0:b0 + ch, 0:lb, :]
        for k in range(1, taps):
            acc = acc + cw_ref[k:k + 1, :][None] * buf[b0:b0 + ch, k:k + lb, :]
        outs.append(acc)
    ns_ref[...] = buf[:, lb:lb + taps - 1, :]
    out = outs[0] if len(outs) == 1 else jnp.concatenate(outs, axis=0)
    return out.reshape(bb * lb, out.shape[-1])


def _mmconv_kernel(*refs, mode, geo, taps, has_state):
    pair = mode in ("glu", "gate")
    two_conv = mode == "gate"
    it = iter(refs)
    h_ref = next(it)
    wa_ref = next(it)
    wb_ref = next(it) if pair else None
    cwa_ref, cba_ref = next(it), next(it)
    cwb_ref, cbb_ref = (next(it), next(it)) if two_conv else (None, None)
    sta_ref = next(it) if has_state else None
    stb_ref = next(it) if (has_state and two_conv) else None
    o_ref = next(it)
    nsa_ref = next(it)
    nsb_ref = next(it) if two_conv else None
    was = next(it)
    wbs = next(it) if pair else None
    bufa = next(it)
    bufb = next(it) if two_conv else None

    i = pl.program_id(1)

    @pl.when(i == 0)
    def _():
        was[...] = wa_ref[...].astype(BF16)
        if pair:
            wbs[...] = wb_ref[...].astype(BF16)

    h = h_ref[...]
    a = jnp.dot(h, was[...], preferred_element_type=F32)
    if pair:
        b = jnp.dot(h, wbs[...], preferred_element_type=F32)
    if mode == "glu":
        a = a * jax.nn.sigmoid(b)

    def conv(buf, u, st_ref, cw_ref, cb_ref, ns_ref):
        if has_state:
            return _conv_short(buf, u, st_ref, cw_ref, cb_ref, ns_ref, taps, geo.bb, geo.lb)
        return _conv_long(buf, u, cw_ref, cb_ref, ns_ref, geo.ti(i), geo.nt, taps, geo.tm)

    ca = conv(bufa, a, sta_ref, cwa_ref, cba_ref, nsa_ref)
    if mode == "glu":
        out = ca
    elif mode == "act":
        out = _silu(ca)
    else:
        cb = conv(bufb, b, stb_ref, cwb_ref, cbb_ref, nsb_ref)
        out = ca * _silu(cb)
    o_ref[...] = out.astype(o_ref.dtype)


def _mmconv(h, w, col_a, col_b, cw, cb, ccol_a, ccol_b, state, n_out, tn, mode, geo, out_dtype, name):
    T, K = h.shape
    taps, C = cw.shape
    pair = mode in ("glu", "gate")
    two_conv = mode == "gate"
    has_state = state is not None
    B = geo.B
    nj = n_out // tn
    cb2 = cb.reshape(1, C)

    in_specs = [pl.BlockSpec((geo.tm, K), lambda j, i: (i, 0)),
                pl.BlockSpec((K, tn), lambda j, i: (0, col_a + j))]
    args = [h, w]
    if pair:
        in_specs.append(pl.BlockSpec((K, tn), lambda j, i: (0, col_b + j)))
        args.append(w)
    in_specs += [pl.BlockSpec((taps, tn), lambda j, i: (0, ccol_a + j)),
                 pl.BlockSpec((1, tn), lambda j, i: (0, ccol_a + j))]
    args += [cw, cb2]
    if two_conv:
        in_specs += [pl.BlockSpec((taps, tn), lambda j, i: (0, ccol_b + j)),
                     pl.BlockSpec((1, tn), lambda j, i: (0, ccol_b + j))]
        args += [cw, cb2]
    if has_state:
        in_specs.append(pl.BlockSpec((geo.bb, taps - 1, tn), lambda j, i: (geo.bi(i), 0, ccol_a + j)))
        args.append(state)
        if two_conv:
            in_specs.append(pl.BlockSpec((geo.bb, taps - 1, tn), lambda j, i: (geo.bi(i), 0, ccol_b + j)))
            args.append(state)

    out_shape = [jax.ShapeDtypeStruct((T, n_out), out_dtype)]
    out_specs = [pl.BlockSpec((geo.tm, tn), lambda j, i: (i, j))]
    for _ in range(2 if two_conv else 1):
        out_shape.append(jax.ShapeDtypeStruct((B, taps - 1, n_out), F32))
        out_specs.append(pl.BlockSpec((geo.bb, taps - 1, tn), lambda j, i: (geo.bi(i), 0, j)))

    if has_state:
        buf_shape = (geo.bb, _round_up(taps - 1 + geo.lb, SUBLANES), tn)
    else:
        buf_shape = (_round_up(taps - 1, SUBLANES) + geo.tm, tn)
    scratch = [pltpu.VMEM((K, tn), BF16)]
    if pair:
        scratch.append(pltpu.VMEM((K, tn), BF16))
    scratch.append(pltpu.VMEM(buf_shape, F32))
    if two_conv:
        scratch.append(pltpu.VMEM(buf_shape, F32))

    kern = functools.partial(_mmconv_kernel, mode=mode, geo=geo, taps=taps, has_state=has_state)
    return pl.pallas_call(
        kern, out_shape=out_shape, grid=(nj, geo.n_tiles),
        in_specs=in_specs, out_specs=out_specs, scratch_shapes=scratch,
        compiler_params=_params(2), name=name,
    )(*args)


def _mm_kernel(h_ref, w_ref, b_ref, o_ref, ws, *, epilogue):
    @pl.when(pl.program_id(1) == 0)
    def _():
        ws[...] = w_ref[...].astype(BF16)

    a = jnp.dot(h_ref[...], ws[...], preferred_element_type=F32)
    if epilogue == "silu":
        o_ref[...] = _silu(a)
    else:
        v = a + b_ref[...]
        o_ref[...] = jnp.maximum(v, 0.0) + jnp.log1p(jnp.exp(-jnp.abs(v)))


def _mm(h, w, col, bias, n_out, tn, epilogue, geo, name):
    T, K = h.shape
    return pl.pallas_call(
        functools.partial(_mm_kernel, epilogue=epilogue),
        out_shape=jax.ShapeDtypeStruct((T, n_out), F32),
        grid=(n_out // tn, geo.n_tiles),
        in_specs=[pl.BlockSpec((geo.tm, K), lambda j, i: (i, 0)),
                  pl.BlockSpec((K, tn), lambda j, i: (0, col + j)),
                  pl.BlockSpec((1, tn), lambda j, i: (0, j))],
        out_specs=pl.BlockSpec((geo.tm, tn), lambda j, i: (i, j)),
        scratch_shapes=[pltpu.VMEM((K, tn), BF16)],
        compiler_params=_params(2), name=name,
    )(h, w, bias)


def _ssd_kernel(*refs, Q, nseq, H, has_state):
    it = iter(refs)
    xbc_ref, dt_ref, zs_ref, alog_ref, dsk_ref, nw_ref, e2_ref = (next(it) for _ in range(7))
    st_ref = next(it) if has_state else None
    y_ref, ns_ref = next(it), next(it)

    P, G, N = SSM_HEAD_DIM, SSM_GROUPS, SSM_STATE
    R = H // G
    GP = R * P
    DS = H * P
    Ls = Q // nseq
    c = pl.program_id(1)

    @pl.when(c == 0)
    def _():
        if has_state:
            ns_ref[...] = st_ref[...]
        else:
            ns_ref[...] = jnp.zeros(ns_ref.shape, F32)

    xbc = xbc_ref[...]
    xs = xbc[:, :DS]
    dt = dt_ref[...]
    lane = lax.broadcasted_iota(jnp.int32, (1, LANES), 1)
    a = jnp.where(lane < H, -jnp.exp(alog_ref[...]), 0.0)
    dA = dt * a

    ri = lax.broadcasted_iota(jnp.int32, (Q, Q), 0)
    ci = lax.broadcasted_iota(jnp.int32, (Q, Q), 1)
    if nseq > 1:
        sh = int(math.log2(Ls))
        assert (1 << sh) == Ls
        same = jnp.right_shift(ri, sh) == jnp.right_shift(ci, sh)
        mask = jnp.logical_and(ri >= ci, same)
        same_f = same.astype(F32)
    else:
        mask = ri >= ci
        same_f = jnp.ones((Q, Q), F32)
    hp = lax.Precision.HIGHEST
    acs = jnp.dot(mask.astype(F32), dA, precision=hp, preferred_element_type=F32)
    acs_end = jnp.dot(same_f, dA, precision=hp, preferred_element_type=F32)
    acsT = acs.T
    e_end = jnp.exp(acs_end)

    stack = jnp.concatenate([dt, jnp.exp(acs), jnp.exp(acs_end - acs)], axis=0)
    hi = stack.astype(BF16)
    lo = (stack - hi.astype(F32)).astype(BF16)
    ex = jnp.dot(jnp.concatenate([hi, lo], axis=1), e2_ref[...], preferred_element_type=F32)
    dtx, eax, dendx = ex[0:Q], ex[Q:2 * Q], ex[2 * Q:3 * Q]
    xdt = xs * dtx
    xdec = (xdt * dendx).astype(BF16)
    xdt_b = xdt.astype(BF16)
    lane_q = lax.broadcasted_iota(jnp.int32, (Q, 2 * P), 1)
    row_q = lax.broadcasted_iota(jnp.int32, (Q, N), 0)

    y_parts = []
    for g in range(G):
        bg = xbc[:, DS + g * N:DS + (g + 1) * N]
        cg = xbc[:, DS + (G + g) * N:DS + (G + g + 1) * N]
        bg_b = bg.astype(BF16)
        cg_b = cg.astype(BF16)
        cb = lax.dot_general(cg_b, bg_b, (((1,), (1,)), ((), ())), preferred_element_type=F32)
        for r2 in range(R // 2):
            yp = None
            for half in range(2):
                hh = g * R + 2 * r2 + half
                col = jnp.broadcast_to(acs[:, hh:hh + 1], (Q, Q))
                row = jnp.broadcast_to(acsT[hh:hh + 1, :], (Q, Q))
                lm = jnp.where(mask, jnp.exp(col - row), 0.0)
                m = (cb * lm).astype(BF16)
                xpair = xdt_b[:, (hh - half) * P:(hh - half + 2) * P]
                keep = (lane_q >= P) if half else (lane_q < P)
                xsel = jnp.where(keep, xpair, jnp.zeros_like(xpair))
                t = jnp.dot(m, xsel, preferred_element_type=F32)
                yp = t if yp is None else yp + t
            y_parts.append(yp)
        yoffs = []
        for s in range(nseq):
            sg = ns_ref[s, g]
            yoffs.append(lax.dot_general(cg_b[s * Ls:(s + 1) * Ls], sg.astype(BF16),
                                         (((1,), (1,)), ((), ())), preferred_element_type=F32))
        yoff = yoffs[0] if nseq == 1 else jnp.concatenate(yoffs, axis=0)
        y_parts.append(("off", g, yoff))
        xdec_g = xdec[:, g * GP:(g + 1) * GP]
        for s in range(nseq):
            if nseq == 1:
                bs = bg_b
            else:
                in_s = jnp.logical_and(row_q >= s * Ls, row_q < (s + 1) * Ls)
                bs = jnp.where(in_s, bg_b, jnp.zeros_like(bg_b))
            upd = lax.dot_general(xdec_g, bs, (((0,), (0,)), ((), ())), preferred_element_type=F32)
            for r in range(R):
                hh = g * R + r
                dec = e_end[s * Ls:s * Ls + 1, hh:hh + 1]
                ns_ref[s, g, r * P:(r + 1) * P, :] = (ns_ref[s, g, r * P:(r + 1) * P, :] * dec
                                                      + upd[r * P:(r + 1) * P, :])

    diag_cols, off_cols = [], []
    for part in y_parts:
        if isinstance(part, tuple):
            off_cols.append(part[2])
        else:
            diag_cols.append(part)
    y = (jnp.concatenate(diag_cols, axis=1) + jnp.concatenate(off_cols, axis=1) * eax
         + xs * dsk_ref[...])
    yz = y * zs_ref[...]
    y_ref[...] = yz * lax.rsqrt(jnp.mean(yz * yz, axis=-1, keepdims=True) + RMS_EPS) * nw_ref[...]


def _ssd(xbc_act, dt, zs, a_log_pad, dsk_x, norm_w, e2, state, B, L):
    T, DXBC = xbc_act.shape
    DS = zs.shape[1]
    H = DS // SSM_HEAD_DIM
    G, N, P = SSM_GROUPS, SSM_STATE, SSM_HEAD_DIM
    GP = (H // G) * P
    has_state = state is not None
    if L % SSD_CHUNK == 0:
        Q, nseq = SSD_CHUNK, 1
    else:
        nseq = max(1, min(B, 64 // L))
        Q = nseq * L
        assert B % nseq == 0
    nb, nc = B // nseq, (L * nseq) // Q
    in_specs = [pl.BlockSpec((Q, DXBC), lambda b, c: (b * nc + c, 0)),
                pl.BlockSpec((Q, LANES), lambda b, c: (b * nc + c, 0)),
                pl.BlockSpec((Q, DS), lambda b, c: (b * nc + c, 0)),
                pl.BlockSpec((1, LANES), lambda b, c: (0, 0)),
                pl.BlockSpec((1, DS), lambda b, c: (0, 0)),
                pl.BlockSpec((1, DS), lambda b, c: (0, 0)),
                pl.BlockSpec((2 * LANES, DS), lambda b, c: (0, 0))]
    args = [xbc_act, dt, zs, a_log_pad, dsk_x, norm_w.reshape(1, DS), e2]
    st_spec = pl.BlockSpec((nseq, G, GP, N), lambda b, c: (b, 0, 0, 0))
    if has_state:
        in_specs.append(st_spec)
        args.append(state.reshape(B, G, GP, N))
    y, ns = pl.pallas_call(
        functools.partial(_ssd_kernel, Q=Q, nseq=nseq, H=H, has_state=has_state),
        out_shape=[jax.ShapeDtypeStruct((T, DS), F32), jax.ShapeDtypeStruct((B, G, GP, N), F32)],
        grid=(nb, nc), in_specs=in_specs,
        out_specs=[pl.BlockSpec((Q, DS), lambda b, c: (b * nc + c, 0)), st_spec],
        compiler_params=_params(2), name="ssd_scan",
    )(*args)
    return y, ns.reshape(B, H, P, N)


def _outproj_kernel(v_ref, lnw_ref, lnb_ref, y_ref, wt_ref, wb_ref, x_ref, g_ref, o_ref, wts, wbs):
    @pl.when(pl.program_id(1) == 0)
    def _():
        wts[...] = wt_ref[...].astype(BF16)
        wbs[...] = wb_ref[...].astype(BF16)

    v = v_ref[...]
    xc = v - jnp.mean(v, axis=-1, keepdims=True)
    ln = xc * lax.rsqrt(jnp.mean(xc * xc, axis=-1, keepdims=True) + LN_EPS) * lnw_ref[...] + lnb_ref[...]
    out = (jnp.dot(_silu(ln).astype(BF16), wts[...], preferred_element_type=F32)
           + jnp.dot(y_ref[...].astype(BF16), wbs[...], preferred_element_type=F32))
    o_ref[...] = x_ref[...] + g_ref[...] * out.reshape(o_ref.shape)


def _outproj(v_pre, ln_w, ln_b, y, w_out, x, mod, g_idx, tn, geo):
    T, DC = v_pre.shape
    DS = y.shape[1]
    B, L, D = x.shape
    assert DC == DS
    gcol = g_idx * (D // tn)
    xspec = pl.BlockSpec((geo.bb, geo.lb, tn), lambda j, i: (geo.bi(i), geo.ti(i), j))
    return pl.pallas_call(
        _outproj_kernel,
        out_shape=jax.ShapeDtypeStruct((B, L, D), F32),
        grid=(D // tn, geo.n_tiles),
        in_specs=[pl.BlockSpec((geo.tm, DC), lambda j, i: (i, 0)),
                  pl.BlockSpec((1, DC), lambda j, i: (0, 0)),
                  pl.BlockSpec((1, DC), lambda j, i: (0, 0)),
                  pl.BlockSpec((geo.tm, DS), lambda j, i: (i, 0)),
                  pl.BlockSpec((DC, tn), lambda j, i: (0, j)),
                  pl.BlockSpec((DS, tn), lambda j, i: (1, j)),
                  xspec,
                  pl.BlockSpec((geo.bb, 1, tn), lambda j, i: (geo.bi(i), 0, gcol + j))],
        out_specs=xspec,
        scratch_shapes=[pltpu.VMEM((DC, tn), BF16), pltpu.VMEM((DS, tn), BF16)],
        compiler_params=_params(2), name="out_proj",
    )(v_pre, ln_w.reshape(1, DC), ln_b.reshape(1, DC), y, w_out, w_out, x, mod)


def _down_kernel(a_ref, w_ref, x_ref, g_ref, o_ref, ws):
    @pl.when(pl.program_id(1) == 0)
    def _():
        ws[...] = w_ref[...].astype(BF16)

    out = jnp.dot(a_ref[...], ws[...], preferred_element_type=F32)
    o_ref[...] = x_ref[...] + g_ref[...] * out.reshape(o_ref.shape)


def _down(act, w, x, mod, g_idx, tn, geo):
    T, K = act.shape
    B, L, D = x.shape
    gcol = g_idx * (D // tn)
    xspec = pl.BlockSpec((geo.bb, geo.lb, tn), lambda j, i: (geo.bi(i), geo.ti(i), j))
    return pl.pallas_call(
        _down_kernel,
        out_shape=jax.ShapeDtypeStruct((B, L, D), F32),
        grid=(D // tn, geo.n_tiles),
        in_specs=[pl.BlockSpec((geo.tm, K), lambda j, i: (i, 0)),
                  pl.BlockSpec((K, tn), lambda j, i: (0, j)),
                  xspec,
                  pl.BlockSpec((geo.bb, 1, tn), lambda j, i: (geo.bi(i), 0, gcol + j))],
        out_specs=xspec,
        scratch_shapes=[pltpu.VMEM((K, tn), BF16)],
        compiler_params=_params(2), name="ffn_down",
    )(act, w, x, mod)


def _col_tile(n, want):
    t = min(want, n)
    while n % t:
        t //= 2
    return t


def _run_trunk(x, mod_all, row0, states, p, consts):
    B, L, D = x.shape
    depth = p["w_in"].shape[0]
    DC = p["conf_dw_w"].shape[2]
    DXBC = p["ssm_conv_w"].shape[2]
    DFF2 = p["ffn_dw_w"].shape[2]
    DFF = DFF2 // 2
    DS = p["ssm_norm_w"].shape[1]
    H = DS // SSM_HEAD_DIM
    geo = Geo(B, L, min(TOKEN_TILE, B * L))
    geo_small = Geo(B, L, min(TOKEN_TILE // 2, B * L))
    tn_c = _col_tile(DC, 512)
    tn_x = _col_tile(DXBC, 512)
    tn_f = _col_tile(DFF, 512)
    tn_d = _col_tile(D, 512)
    new_conf, new_sconv, new_ssm, new_ffn = [], [], [], []
    for l in range(depth):
        mod = lax.slice_in_dim(mod_all[l], row0, row0 + B, axis=0).reshape(B, 1, N_MOD * D)
        st_conf, st_sconv, st_ssm, st_ffn = (None,) * 4 if states is None else (s[l] for s in states)
        w_in = p["w_in"][l]
        h = _normmod(x, p["norm_mix_w"][l], mod, 1, 0, geo)
        v_pre, ns_conf = _mmconv(h, w_in, 0, DC // tn_c, p["conf_dw_w"][l], p["conf_dw_b"][l], 0, 0,
                                 st_conf, DC, tn_c, "glu", geo, F32, "in_conf")
        zs = _mm(h, w_in, 2 * DC // tn_c, consts["zero_bias"], DS, tn_c, "silu", geo, "in_z")
        xcol = (2 * DC + DS) // tn_x
        xbc_act, ns_sconv = _mmconv(h, w_in, xcol, 0, p["ssm_conv_w"][l], p["ssm_conv_b"][l], 0, 0,
                                    st_sconv, DXBC, tn_x, "act", geo, F32, "in_xbc")
        dt = _mm(h, consts["w_dt"][l], 0, consts["dt_bias"][l], LANES, LANES, "softplus", geo, "in_dt")
        y, ns_ssm = _ssd(xbc_act, dt, zs, consts["a_log"][l], consts["d_skip"][l], p["ssm_norm_w"][l],
                         consts["e2"], st_ssm, B, L)
        x = _outproj(v_pre, p["conf_ln_w"][l], p["conf_ln_b"][l], y, p["w_out"][l], x, mod, 2, tn_d, geo_small)
        h = _normmod(x, p["norm_ffn_w"][l], mod, 4, 3, geo)
        act, ns_fa, ns_fb = _mmconv(h, p["ffn_w_up"][l], 0, DFF // tn_f, p["ffn_dw_w"][l], p["ffn_dw_b"][l],
                                    0, DFF // tn_f, st_ffn, DFF, tn_f, "gate", geo, BF16, "ffn_up")
        ns_ffn = jnp.concatenate([ns_fa, ns_fb], axis=-1)
        x = _down(act, p["ffn_w_down"][l], x, mod, 5, tn_d, geo)
        new_conf.append(ns_conf)
        new_sconv.append(ns_sconv)
        new_ssm.append(ns_ssm)
        new_ffn.append(ns_ffn)
    y = _final_norm(x, p["final_norm_w"], geo)
    return y, jnp.stack(new_conf), jnp.stack(new_sconv), jnp.stack(new_ssm), jnp.stack(new_ffn)


def kernel(x_prompt, x_sample, c_prompt, c_sample, state_conf_conv, state_ssm_conv, state_ssm, state_ffn_conv, ada_w, ada_b, norm_mix_w, norm_ffn_w, w_in, conf_dw_w, conf_dw_b, conf_ln_w, conf_ln_b, ssm_conv_w, ssm_conv_b, dt_bias, a_log, d_skip, ssm_norm_w, w_out, ffn_w_up, ffn_dw_w, ffn_dw_b, ffn_w_down, final_norm_w):
    p = dict(norm_mix_w=norm_mix_w, norm_ffn_w=norm_ffn_w, w_in=w_in, conf_dw_w=conf_dw_w,
             conf_dw_b=conf_dw_b, conf_ln_w=conf_ln_w, conf_ln_b=conf_ln_b, ssm_conv_w=ssm_conv_w,
             ssm_conv_b=ssm_conv_b, ssm_norm_w=ssm_norm_w, w_out=w_out, ffn_w_up=ffn_w_up,
             ffn_dw_w=ffn_dw_w, ffn_dw_b=ffn_dw_b, ffn_w_down=ffn_w_down, final_norm_w=final_norm_w)
    depth, D, d_in = w_in.shape
    H = a_log.shape[1]
    DS = ssm_norm_w.shape[1]
    bp, bs = x_prompt.shape[0], x_sample.shape[0]

    pad_h = LANES - H
    head_of_lane = jnp.arange(DS, dtype=jnp.int32) // SSM_HEAD_DIM
    e1 = (jnp.arange(LANES, dtype=jnp.int32)[:, None] == head_of_lane[None, :]).astype(BF16)
    consts = dict(
        w_dt=jnp.pad(w_in[:, :, d_in - H:], ((0, 0), (0, 0), (0, pad_h))),
        dt_bias=jnp.pad(dt_bias, ((0, 0), (0, pad_h))).reshape(depth, 1, LANES),
        a_log=jnp.pad(a_log, ((0, 0), (0, pad_h))).reshape(depth, 1, LANES),
        d_skip=jnp.repeat(d_skip, SSM_HEAD_DIM, axis=1).reshape(depth, 1, DS),
        e2=jnp.concatenate([e1, e1], axis=0),
        zero_bias=jnp.zeros((1, DS), F32),
    )

    rows = bp + bs
    c_all = jnp.pad(jnp.concatenate([c_prompt, c_sample], axis=0), ((0, _round_up(rows, SUBLANES) - rows), (0, 0)))
    mod_all = _ada(c_all, ada_w, ada_b)

    y_p, p_conf, p_sconv, p_ssm, p_ffn = _run_trunk(x_prompt, mod_all, 0, None, p, consts)
    y_s, s_conf, s_sconv, s_ssm, s_ffn = _run_trunk(
        x_sample, mod_all, bp, (state_conf_conv, state_ssm_conv, state_ssm, state_ffn_conv), p, consts)
    return (y_p, y_s, p_conf, p_sconv, p_ssm, p_ffn, s_conf, s_sconv, s_ssm, s_ffn)
```
